```python
import jax, jax.numpy as jnp
from jax import lax
import numpy as np

D_MODEL = 1024
BATCH = 4
SEQ = 8192
DEPTH = 1

EPS = 1e-6
N_MOD = 6
CONV_WIDTH = D_MODEL
CONV_KERNEL = 31
MLSTM_HEADS = 4
MLSTM_HEAD_DIM = D_MODEL // MLSTM_HEADS
MLSTM_WIDTH = MLSTM_HEADS * MLSTM_HEAD_DIM
QK_CONV = 4
CHUNK = 64
D_FF = ((8 * D_MODEL + 3 * 256 - 1) // (3 * 256)) * 256

N_GLU = 2 * CONV_WIDTH
N_QK = 2 * MLSTM_WIDTH
N_V = MLSTM_WIDTH
N_O = MLSTM_WIDTH
N_IF = 2 * MLSTM_HEADS
N_MERGE = 2 * D_MODEL
N_IN = N_GLU + N_QK + N_V + N_O + N_IF + N_MERGE
IN_SPLITS = (N_GLU, N_GLU + N_QK, N_GLU + N_QK + N_V, N_GLU + N_QK + N_V + N_O,
             N_GLU + N_QK + N_V + N_O + N_IF)
F_BIAS_OFFSET = N_GLU + N_QK + N_V + N_O + MLSTM_HEADS

kernel_name = "hybrid_conformer_mlstm_adaln_block"


def rms_norm(x, g):
    xf = x.astype(jnp.float32)
    y = xf * lax.rsqrt(jnp.mean(xf * xf, axis=-1, keepdims=True) + EPS)
    return (y * g.astype(jnp.float32)).astype(x.dtype)


def layer_norm(x, g, b):
    xf = x.astype(jnp.float32)
    mu = jnp.mean(xf, axis=-1, keepdims=True)
    var = jnp.mean(jnp.square(xf - mu), axis=-1, keepdims=True)
    y = (xf - mu) * lax.rsqrt(var + EPS)
    return (y * g.astype(jnp.float32) + b.astype(jnp.float32)).astype(x.dtype)


def causal_depthwise_conv(x, w, b):
    k = w.shape[0]
    y = lax.conv_general_dilated(
        x, w[:, None, :].astype(x.dtype), window_strides=(1,), padding=((k - 1, 0),),
        dimension_numbers=("NWC", "WIO", "NWC"), feature_group_count=x.shape[-1])
    return y + b


def mlstm_chunkwise(q, k, v, i_pre, f_pre):
    bsz, nh, s, dh = q.shape
    nc = s // CHUNK

    def to_chunks(t):
        return jnp.moveaxis(t.reshape(t.shape[:2] + (nc, CHUNK) + t.shape[3:]), 2, 0)

    log_f = jax.nn.log_sigmoid(f_pre)
    xs = (to_chunks(q), to_chunks(k), to_chunks(v), to_chunks(i_pre), to_chunks(log_f))
    causal = jnp.tril(jnp.ones((CHUNK, CHUNK), dtype=bool))

    def step(carry, inp):
        c_st, n_st, m_st = carry
        qc, kc, vc, ic, lfc = inp
        b = jnp.cumsum(lfc, axis=-1)
        d_log = jnp.where(causal, b[..., :, None] - b[..., None, :] + ic[..., None, :], -jnp.inf)
        inter_log = b + m_st[..., None]
        m_t = jnp.maximum(inter_log, jnp.max(d_log, axis=-1))
        dmat = jnp.exp(d_log - m_t[..., None])
        inter_w = jnp.exp(inter_log - m_t)
        sc = jnp.einsum("bhtd,bhsd->bhts", qc, kc) * dmat
        num = (jnp.einsum("bhts,bhse->bhte", sc, vc)
               + inter_w[..., None] * jnp.einsum("bhtd,bhde->bhte", qc, c_st))
        qn = jnp.sum(sc, axis=-1) + inter_w * jnp.einsum("bhtd,bhd->bht", qc, n_st)
        h = num / jnp.maximum(jnp.abs(qn), jnp.exp(-m_t))[..., None]
        b_last = b[..., -1]
        a = b_last[..., None] - b + ic
        m_new = jnp.maximum(b_last + m_st, jnp.max(a, axis=-1))
        decay = jnp.exp(b_last + m_st - m_new)
        w = jnp.exp(a - m_new[..., None])
        c_new = decay[..., None, None] * c_st + jnp.einsum("bhs,bhsd,bhse->bhde", w, kc, vc)
        n_new = decay[..., None] * n_st + jnp.einsum("bhs,bhsd->bhd", w, kc)
        return (c_new, n_new, m_new), h

    init = (jnp.zeros((bsz, nh, dh, dh), jnp.float32),
            jnp.zeros((bsz, nh, dh), jnp.float32),
            jnp.zeros((bsz, nh), jnp.float32))
    _, h = lax.scan(step, init, xs)
    return jnp.moveaxis(h, 0, 2).reshape(bsz, nh, s, dh)


def setup_inputs(seed: int = 0) -> dict:
    key = jax.random.key(seed)
    ks = jax.random.split(key, 24)
    f32 = jnp.float32

    def nrm(k, shape, scale):
        return jax.random.normal(k, shape, f32) * scale

    x = jax.random.normal(ks[0], (BATCH, SEQ, D_MODEL), f32)
    c = jax.random.normal(ks[1], (BATCH, D_MODEL), f32)
    w_ada = nrm(ks[2], (DEPTH, D_MODEL, N_MOD * D_MODEL), 0.5 * D_MODEL ** -0.5)
    b_ada = nrm(ks[3], (DEPTH, N_MOD * D_MODEL), 0.02)
    norm_mix_g = 1.0 + nrm(ks[4], (DEPTH, D_MODEL), 0.1)
    w_in = nrm(ks[5], (DEPTH, D_MODEL, N_IN), D_MODEL ** -0.5)
    f_bias = jnp.linspace(3.0, 6.0, MLSTM_HEADS, dtype=f32)
    b_in = nrm(ks[6], (DEPTH, N_IN), 0.02)
    b_in = b_in.at[:, F_BIAS_OFFSET:F_BIAS_OFFSET + MLSTM_HEADS].add(f_bias)
    conv_dw_w = nrm(ks[7], (DEPTH, CONV_KERNEL, CONV_WIDTH), CONV_KERNEL ** -0.5)
    conv_dw_b = nrm(ks[8], (DEPTH, CONV_WIDTH), 0.02)
    conv_ln_g = 1.0 + nrm(ks[9], (DEPTH, CONV_WIDTH), 0.1)
    conv_ln_b = nrm(ks[10], (DEPTH, CONV_WIDTH), 0.02)
    w_conv_pw = nrm(ks[11], (DEPTH, CONV_WIDTH, D_MODEL), CONV_WIDTH ** -0.5)
    b_conv_pw = nrm(ks[12], (DEPTH, D_MODEL), 0.02)
    qk_conv_w = nrm(ks[13], (DEPTH, QK_CONV, N_QK), QK_CONV ** -0.5)
    qk_conv_b = nrm(ks[14], (DEPTH, N_QK), 0.02)
    mlstm_norm_g = 1.0 + nrm(ks[15], (DEPTH, MLSTM_WIDTH), 0.1)
    w_out = nrm(ks[16], (DEPTH, D_MODEL, D_MODEL), D_MODEL ** -0.5)
    norm_ffn_g = 1.0 + nrm(ks[17], (DEPTH, D_MODEL), 0.1)
    w_ffn_up = nrm(ks[18], (DEPTH, D_MODEL, 2 * D_FF), D_MODEL ** -0.5)
    w_ffn_down = nrm(ks[19], (DEPTH, D_FF, D_MODEL), D_FF ** -0.5)
    final_norm_g = 1.0 + nrm(ks[20], (D_MODEL,), 0.1)
    return {"x": x, "c": c, "w_ada": w_ada, "b_ada": b_ada, "norm_mix_g": norm_mix_g,
            "w_in": w_in, "b_in": b_in, "conv_dw_w": conv_dw_w, "conv_dw_b": conv_dw_b,
            "conv_ln_g": conv_ln_g, "conv_ln_b": conv_ln_b, "w_conv_pw": w_conv_pw,
            "b_conv_pw": b_conv_pw, "qk_conv_w": qk_conv_w, "qk_conv_b": qk_conv_b,
            "mlstm_norm_g": mlstm_norm_g, "w_out": w_out, "norm_ffn_g": norm_ffn_g,
            "w_ffn_up": w_ffn_up, "w_ffn_down": w_ffn_down, "final_norm_g": final_norm_g}


def reference(x, c, w_ada, b_ada, norm_mix_g, w_in, b_in, conv_dw_w, conv_dw_b,
              conv_ln_g, conv_ln_b, w_conv_pw, b_conv_pw, qk_conv_w, qk_conv_b,
              mlstm_norm_g, w_out, norm_ffn_g, w_ffn_up, w_ffn_down, final_norm_g):
    bsz, s, _ = x.shape
    for l in range(DEPTH):
        mod = jnp.einsum("bd,de->be", jax.nn.silu(c), w_ada[l]) + b_ada[l]
        sh1, sc1, g1, sh2, sc2, g2 = jnp.split(mod[:, None, :], N_MOD, axis=-1)

        h = rms_norm(x, norm_mix_g[l]) * (1.0 + sc1) + sh1
        proj = jnp.einsum("bsd,de->bse", h, w_in[l]) + b_in[l]
        glu_in, qk_in, v, o_pre, if_pre, merge_pre = jnp.split(proj, IN_SPLITS, axis=-1)

        a = glu_in[..., :CONV_WIDTH] * jax.nn.sigmoid(glu_in[..., CONV_WIDTH:])
        a = causal_depthwise_conv(a, conv_dw_w[l], conv_dw_b[l])
        a = jax.nn.silu(layer_norm(a, conv_ln_g[l], conv_ln_b[l]))
        y_conv = jnp.einsum("bsc,cd->bsd", a, w_conv_pw[l]) + b_conv_pw[l]

        qk = jax.nn.silu(causal_depthwise_conv(qk_in, qk_conv_w[l], qk_conv_b[l]))
        q, k = jnp.split(qk, 2, axis=-1)

        def heads(t):
            return t.reshape(bsz, s, MLSTM_HEADS, MLSTM_HEAD_DIM).transpose(0, 2, 1, 3).astype(jnp.float32)

        gates = jnp.moveaxis(if_pre, -1, 1).astype(jnp.float32)
        hm = mlstm_chunkwise(heads(q), heads(k) * (MLSTM_HEAD_DIM ** -0.5), heads(v),
                             gates[:, :MLSTM_HEADS], gates[:, MLSTM_HEADS:])
        mu = jnp.mean(hm, axis=-1, keepdims=True)
        var = jnp.mean(jnp.square(hm - mu), axis=-1, keepdims=True)
        hm = ((hm - mu) * lax.rsqrt(var + EPS)).transpose(0, 2, 1, 3).reshape(bsz, s, MLSTM_WIDTH)
        hm = (hm * mlstm_norm_g[l].astype(jnp.float32)).astype(x.dtype)
        y_mlstm = jax.nn.sigmoid(o_pre) * hm

        g_conv, g_mlstm = jnp.split(jax.nn.sigmoid(merge_pre), 2, axis=-1)
        y = jnp.einsum("bsd,de->bse", g_conv * y_conv + g_mlstm * y_mlstm, w_out[l])
        x = x + g1 * y

        h = rms_norm(x, norm_ffn_g[l]) * (1.0 + sc2) + sh2
        gt, up = jnp.split(jnp.einsum("bsd,df->bsf", h, w_ffn_up[l]), 2, axis=-1)
        x = x + g2 * jnp.einsum("bsf,fd->bsd", jax.nn.silu(gt) * up, w_ffn_down[l])
    return rms_norm(x, final_norm_g)
```

```python
import functools

import jax
import jax.numpy as jnp
from jax import lax
from jax.experimental import pallas as pl
from jax.experimental.pallas import tpu as pltpu

F32 = jnp.float32
BF16 = jnp.bfloat16

EPS = 1e-6
N_MOD = 6
CONV_KERNEL = 31
MLSTM_HEADS = 4
QK_CONV = 4

V7X_VMEM_BYTES = 64 * 1024 * 1024
VMEM_LIMIT_BYTES = V7X_VMEM_BYTES - 8 * 1024 * 1024
LANES = 128
SUBLANES = 8
CONV_HALO = 32
QK_HALO = 8
MLSTM_CHUNK = 256
CONV_ROWS = 64


def _sigmoid(x):
    return 1.0 / (1.0 + jnp.exp(-x))


def _silu(x):
    return x * _sigmoid(x)


def _rms_norm(x, g):
    return x * lax.rsqrt(jnp.mean(x * x, axis=-1, keepdims=True) + EPS) * g


def _params(sem):
    return pltpu.CompilerParams(dimension_semantics=sem, vmem_limit_bytes=VMEM_LIMIT_BYTES)


def _resident(shape):
    nd = len(shape)
    return pl.BlockSpec(shape, lambda *_: (0,) * nd, pipeline_mode=pl.Buffered(1))


def _adaln_kernel(c_ref, w_ref, b_ref, o_ref):
    s = _silu(c_ref[...])
    o_ref[...] = jnp.dot(s, w_ref[...], preferred_element_type=F32,
                         precision=lax.Precision.HIGHEST) + b_ref[...]


def _adaln(c, w, b):
    bsz, d = c.shape
    n = w.shape[1]
    tn = 1536
    return pl.pallas_call(
        _adaln_kernel,
        grid=(n // tn,),
        in_specs=[pl.BlockSpec((bsz, d), lambda j: (0, 0)),
                  pl.BlockSpec((d, tn), lambda j: (0, j)),
                  pl.BlockSpec((1, tn), lambda j: (0, j))],
        out_specs=pl.BlockSpec((bsz, tn), lambda j: (0, j)),
        out_shape=jax.ShapeDtypeStruct((bsz, n), F32),
        compiler_params=_params(("arbitrary",)),
        name="adaln",
    )(c, w, b.reshape(1, n))


def _inproj_kernel(x_ref, mod_ref, g_ref, w_ref, b_ref,
                   a_ref, qk_ref, v_ref, o_ref, mg_ref, if_ref, *, d):
    x = x_ref[...]
    sh = mod_ref[0, 0:1, :]
    sc = mod_ref[0, 1:2, :]
    h = (_rms_norm(x, g_ref[...]) * (1.0 + sc) + sh).astype(BF16)

    def seg(lo, hi):
        return jnp.dot(h, w_ref[:, lo:hi], preferred_element_type=F32) + b_ref[:, lo:hi]

    a_ref[...] = (seg(0, d) * _sigmoid(seg(d, 2 * d))).astype(BF16)
    qk_ref[:, 0:d] = seg(2 * d, 3 * d).astype(BF16)
    qk_ref[:, d:2 * d] = seg(3 * d, 4 * d).astype(BF16)
    v_ref[...] = seg(4 * d, 5 * d).astype(BF16)
    o_ref[...] = _sigmoid(seg(5 * d, 6 * d)).astype(BF16)
    mg_ref[:, 0:d] = _sigmoid(seg(6 * d, 7 * d)).astype(BF16)
    mg_ref[:, d:2 * d] = _sigmoid(seg(7 * d, 8 * d)).astype(BF16)
    if_ref[...] = seg(8 * d, 8 * d + LANES)


def _inproj(x2, mod, g, w, b, *, seq, tm):
    t, d = x2.shape
    nw = w.shape[1]
    tpb = seq // tm
    row = lambda i: (i, 0)
    outs = [(d, BF16), (2 * d, BF16), (d, BF16), (d, BF16), (2 * d, BF16), (LANES, F32)]
    return pl.pallas_call(
        functools.partial(_inproj_kernel, d=d),
        grid=(t // tm,),
        in_specs=[pl.BlockSpec((tm, d), row),
                  pl.BlockSpec((1, N_MOD, d), lambda i: (i // tpb, 0, 0)),
                  _resident((1, d)),
                  _resident((d, nw)),
                  _resident((1, nw))],
        out_specs=[pl.BlockSpec((tm, n), row) for n, _ in outs],
        out_shape=[jax.ShapeDtypeStruct((t, n), dt) for n, dt in outs],
        compiler_params=_params(("arbitrary",)),
        name="inproj",
    )(x2, mod, g, w, b)


def _conv_kernel(a_ref, w_ref, b_ref, lng_ref, lnb_ref, wpw_ref, bpw_ref, y_ref,
                 ext_ref, cv_ref):
    ts, d = a_ref.shape
    nslab = d // LANES

    @pl.when(pl.program_id(1) == 0)
    def _():
        ext_ref[:, 0:CONV_HALO, :] = jnp.zeros((nslab, CONV_HALO, LANES), F32)

    for c in range(nslab):
        ext_ref[c, CONV_HALO:CONV_HALO + ts, :] = a_ref[:, c * LANES:(c + 1) * LANES].astype(F32)

    off = CONV_HALO - (CONV_KERNEL - 1)
    for c in range(nslab):
        lanes = slice(c * LANES, (c + 1) * LANES)
        taps = [w_ref[j:j + 1, lanes] for j in range(CONV_KERNEL)]
        bias = b_ref[:, lanes]

        def body(r, carry, c=c, taps=taps, bias=bias):
            base = pl.multiple_of(r * (2 * CONV_ROWS), 2 * CONV_ROWS)
            acc0 = jnp.broadcast_to(bias, (CONV_ROWS, LANES))
            acc1 = acc0
            for m in range(CONV_KERNEL + 1):
                xv = ext_ref[c, pl.ds(base + off + m, CONV_ROWS, stride=2), :]
                if m < CONV_KERNEL:
                    acc0 = acc0 + taps[m] * xv
                if m >= 1:
                    acc1 = acc1 + taps[m - 1] * xv
            cv_ref[c, pl.ds(base, CONV_ROWS, stride=2), :] = acc0
            cv_ref[c, pl.ds(base + 1, CONV_ROWS, stride=2), :] = acc1
            return carry

        lax.fori_loop(0, ts // (2 * CONV_ROWS), body, 0)

    for c in range(nslab):
        ext_ref[c, 0:CONV_HALO, :] = ext_ref[c, ts:ts + CONV_HALO, :]

    cv = jnp.concatenate([cv_ref[c] for c in range(nslab)], axis=1)
    mu = jnp.mean(cv, axis=-1, keepdims=True)
    cc = cv - mu
    var = jnp.mean(cc * cc, axis=-1, keepdims=True)
    ln = cc * lax.rsqrt(var + EPS) * lng_ref[...] + lnb_ref[...]
    act = _silu(ln).astype(BF16)
    y = jnp.dot(act, wpw_ref[...], preferred_element_type=F32) + bpw_ref[...]
    y_ref[...] = y.astype(BF16)


def _conv_branch(a, w, b, lng, lnb, wpw, bpw, *, bsz, seq, ts):
    t, d = a.shape
    spb = seq // ts
    row = lambda bi, j: (bi * spb + j, 0)
    return pl.pallas_call(
        _conv_kernel,
        grid=(bsz, spb),
        in_specs=[pl.BlockSpec((ts, d), row),
                  _resident((CONV_KERNEL, d)),
                  _resident((1, d)), _resident((1, d)), _resident((1, d)),
                  _resident((d, d)), _resident((1, d))],
        out_specs=pl.BlockSpec((ts, d), row),
        out_shape=jax.ShapeDtypeStruct((t, d), BF16),
        scratch_shapes=[pltpu.VMEM((d // LANES, ts + CONV_HALO, LANES), F32),
                        pltpu.VMEM((d // LANES, ts, LANES), F32)],
        compiler_params=_params(("arbitrary", "arbitrary")),
        name="convbranch",
    )(a, w, b, lng, lnb, wpw, bpw)


def _mlstm_kernel(qk_ref, v_ref, if_ref, o_ref, cw_ref, cb_ref, ng_ref, y_ref,
                  ext_ref, c_ref, n_ref, m_ref, *, dh):
    L = qk_ref.shape[0]
    d = v_ref.shape[1]

    @pl.when(pl.program_id(1) == 0)
    def _():
        ext_ref[0:QK_HALO, :] = jnp.zeros((QK_HALO, 2 * d), F32)
        c_ref[...] = jnp.zeros(c_ref.shape, F32)
        n_ref[...] = jnp.zeros(n_ref.shape, F32)
        m_ref[...] = jnp.zeros(m_ref.shape, F32)

    ext_ref[QK_HALO:QK_HALO + L, :] = qk_ref[...].astype(F32)
    off = QK_HALO - (QK_CONV - 1)
    acc = cb_ref[...] + cw_ref[0:1, :] * ext_ref[off:off + L, :]
    for j in range(1, QK_CONV):
        acc = acc + cw_ref[j:j + 1, :] * ext_ref[off + j:off + j + L, :]
    ext_ref[0:QK_HALO, :] = ext_ref[L:L + QK_HALO, :]
    qk = _silu(acc)
    q = qk[:, 0:d].astype(BF16)
    kf = qk[:, d:2 * d] * (dh ** -0.5)

    g = if_ref[...]
    lf = jnp.minimum(g, 0.0) - jnp.log(1.0 + jnp.exp(-jnp.abs(g)))
    rows = lax.broadcasted_iota(jnp.int32, (L, L), 0)
    cols = lax.broadcasted_iota(jnp.int32, (L, L), 1)
    causal = rows >= cols
    tri = jnp.where(causal, 1.0, 0.0).astype(F32)
    bcum = jnp.dot(tri, lf, preferred_element_type=F32, precision=lax.Precision.HIGHEST)
    bcum_t = bcum.T
    g_t = g.T

    for h in range(MLSTM_HEADS):
        hs = slice(h * dh, (h + 1) * dh)
        fcol = MLSTM_HEADS + h
        bc = bcum[:, fcol:fcol + 1]
        br = bcum_t[fcol:fcol + 1, :]
        ic = g[:, h:h + 1]
        ir = g_t[h:h + 1, :]
        m_prev = m_ref[h:h + 1, 0:1]

        dlog = jnp.where(causal, bc - br + ir, -jnp.inf)
        inter = bc + m_prev
        m_t = jnp.maximum(inter, jnp.max(dlog, axis=-1, keepdims=True))
        dmat = jnp.exp(dlog - m_t)
        inter_w = jnp.exp(inter - m_t)

        qh = q[:, hs]
        kh = kf[:, hs]
        vh = v_ref[:, hs]
        s = lax.dot_general(qh, kh.astype(BF16), (((1,), (1,)), ((), ())),
                            preferred_element_type=F32)
        sc = s * dmat
        c_old = c_ref[h]
        n_old = n_ref[h:h + 1, :]
        num = (jnp.dot(sc.astype(BF16), vh, preferred_element_type=F32)
               + inter_w * jnp.dot(qh, c_old.astype(BF16), preferred_element_type=F32))
        qn = (jnp.sum(sc, axis=-1, keepdims=True)
              + inter_w * jnp.sum(qh.astype(F32) * n_old, axis=-1, keepdims=True))
        hh = num * (1.0 / jnp.maximum(jnp.abs(qn), jnp.exp(-m_t)))

        mu = jnp.mean(hh, axis=-1, keepdims=True)
        hc = hh - mu
        var = jnp.mean(hc * hc, axis=-1, keepdims=True)
        hn = hc * lax.rsqrt(var + EPS) * ng_ref[:, hs]
        y_ref[:, hs] = (o_ref[:, hs].astype(F32) * hn).astype(BF16)

        b_last = bcum[L - 1:L, fcol:fcol + 1]
        a_row = b_last - br + ir
        a_col = b_last - bc + ic
        m_new = jnp.maximum(b_last + m_prev, jnp.max(a_row, axis=-1, keepdims=True))
        decay = jnp.exp(b_last + m_prev - m_new)
        wk = kh * jnp.exp(a_col - m_new)
        c_ref[h] = decay * c_old + lax.dot_general(
            wk.astype(BF16), vh, (((0,), (0,)), ((), ())), preferred_element_type=F32)
        n_ref[h:h + 1, :] = decay * n_old + jnp.sum(wk, axis=0, keepdims=True)
        m_ref[h:h + 1, :] = jnp.broadcast_to(m_new, (1, LANES))


def _mlstm_branch(qk, v, ifp, o, cw, cb, ng, *, bsz, seq):
    t, d = v.shape
    L = MLSTM_CHUNK
    dh = d // MLSTM_HEADS
    spb = seq // L
    row = lambda bi, j: (bi * spb + j, 0)
    return pl.pallas_call(
        functools.partial(_mlstm_kernel, dh=dh),
        grid=(bsz, spb),
        in_specs=[pl.BlockSpec((L, 2 * d), row),
                  pl.BlockSpec((L, d), row),
                  pl.BlockSpec((L, LANES), row),
                  pl.BlockSpec((L, d), row),
                  _resident((QK_CONV, 2 * d)),
                  _resident((1, 2 * d)),
                  _resident((1, d))],
        out_specs=pl.BlockSpec((L, d), row),
        out_shape=jax.ShapeDtypeStruct((t, d), BF16),
        scratch_shapes=[pltpu.VMEM((L + QK_HALO, 2 * d), F32),
                        pltpu.VMEM((MLSTM_HEADS, dh, dh), F32),
                        pltpu.VMEM((SUBLANES, dh), F32),
                        pltpu.VMEM((SUBLANES, LANES), F32)],
        compiler_params=_params(("arbitrary", "arbitrary")),
        name="mlstm",
    )(qk, v, ifp, o, cw, cb, ng)


def _outffn_kernel(x_ref, yc_ref, ym_ref, mg_ref, mod_ref, wout_ref, g2_ref, wup_ref,
                   wdn_ref, gf_ref, o_ref, *, d, dff, fchunk):
    mix = (mg_ref[:, 0:d].astype(F32) * yc_ref[...].astype(F32)
           + mg_ref[:, d:2 * d].astype(F32) * ym_ref[...].astype(F32))
    y = jnp.dot(mix.astype(BF16), wout_ref[...], preferred_element_type=F32)
    g1 = mod_ref[0, 2:3, :]
    sh2 = mod_ref[0, 3:4, :]
    sc2 = mod_ref[0, 4:5, :]
    g2 = mod_ref[0, 5:6, :]
    x1 = x_ref[...] + g1 * y
    h2 = (_rms_norm(x1, g2_ref[...]) * (1.0 + sc2) + sh2).astype(BF16)
    acc = None
    for lo in range(0, dff, fchunk):
        hi = min(lo + fchunk, dff)
        gt = jnp.dot(h2, wup_ref[:, lo:hi], preferred_element_type=F32)
        up = jnp.dot(h2, wup_ref[:, dff + lo:dff + hi], preferred_element_type=F32)
        ff = (_silu(gt) * up).astype(BF16)
        part = jnp.dot(ff, wdn_ref[lo:hi, :], preferred_element_type=F32)
        acc = part if acc is None else acc + part
    x2 = x1 + g2 * acc
    o_ref[...] = _rms_norm(x2, gf_ref[...])


def _outffn(x2, yc, ym, mg, mod, wout, g2, wup, wdn, gf, *, seq, tm):
    t, d = x2.shape
    dff = wdn.shape[0]
    tpb = seq // tm
    row = lambda i: (i, 0)
    return pl.pallas_call(
        functools.partial(_outffn_kernel, d=d, dff=dff, fchunk=768),
        grid=(t // tm,),
        in_specs=[pl.BlockSpec((tm, d), row),
                  pl.BlockSpec((tm, d), row),
                  pl.BlockSpec((tm, d), row),
                  pl.BlockSpec((tm, 2 * d), row),
                  pl.BlockSpec((1, N_MOD, d), lambda i: (i // tpb, 0, 0)),
                  _resident((d, d)),
                  _resident((1, d)),
                  _resident((d, 2 * dff)),
                  _resident((dff, d)),
                  _resident((1, d))],
        out_specs=pl.BlockSpec((tm, d), row),
        out_shape=jax.ShapeDtypeStruct((t, d), F32),
        compiler_params=_params(("arbitrary",)),
        name="outffn",
    )(x2, yc, ym, mg, mod, wout, g2, wup, wdn, gf)


def _block(x, mod, norm_mix_g, w_in, b_in, conv_dw_w, conv_dw_b, conv_ln_g, conv_ln_b,
           w_conv_pw, b_conv_pw, qk_conv_w, qk_conv_b, mlstm_norm_g, w_out, norm_ffn_g,
           w_ffn_up, w_ffn_down, out_g, *, tm, ts):
    bsz, seq, d = x.shape
    heads = MLSTM_HEADS
    n_main = 6 * d
    w = jnp.concatenate(
        [w_in[:, :n_main], w_in[:, n_main + 2 * heads:],
         jnp.pad(w_in[:, n_main:n_main + 2 * heads], ((0, 0), (0, LANES - 2 * heads)))],
        axis=1).astype(BF16)
    b = jnp.concatenate(
        [b_in[:n_main], b_in[n_main + 2 * heads:],
         jnp.pad(b_in[n_main:n_main + 2 * heads], (0, LANES - 2 * heads))]).reshape(1, -1)
    x2 = x.reshape(bsz * seq, d)
    r1 = lambda v: v.reshape(1, -1)

    a, qk, v, o, mg, ifp = _inproj(x2, mod, r1(norm_mix_g), w, b, seq=seq, tm=tm)
    yc = _conv_branch(a, conv_dw_w, r1(conv_dw_b), r1(conv_ln_g), r1(conv_ln_b),
                      w_conv_pw.astype(BF16), r1(b_conv_pw), bsz=bsz, seq=seq, ts=ts)
    ym = _mlstm_branch(qk, v, ifp, o, qk_conv_w, r1(qk_conv_b), r1(mlstm_norm_g),
                       bsz=bsz, seq=seq)
    out = _outffn(x2, yc, ym, mg, mod, w_out.astype(BF16), r1(norm_ffn_g),
                  w_ffn_up.astype(BF16), w_ffn_down.astype(BF16), r1(out_g), seq=seq, tm=tm)
    return out.reshape(bsz, seq, d)


def kernel(x, c, w_ada, b_ada, norm_mix_g, w_in, b_in, conv_dw_w, conv_dw_b, conv_ln_g,
           conv_ln_b, w_conv_pw, b_conv_pw, qk_conv_w, qk_conv_b, mlstm_norm_g, w_out,
           norm_ffn_g, w_ffn_up, w_ffn_down, final_norm_g):
    depth = w_in.shape[0]
    assert depth == 1, "final rmsnorm is fused into the (single) layer's FFN kernel"
    bsz, seq, d = x.shape
    tm = min(512, seq)
    ts = min(512, seq)
    mod = _adaln(c, w_ada[0], b_ada[0]).reshape(bsz, N_MOD, d)
    return _block(x, mod, norm_mix_g[0], w_in[0], b_in[0], conv_dw_w[0], conv_dw_b[0],
                  conv_ln_g[0], conv_ln_b[0], w_conv_pw[0], b_conv_pw[0], qk_conv_w[0],
                  qk_conv_b[0], mlstm_norm_g[0], w_out[0], norm_ffn_g[0], w_ffn_up[0],
                  w_ffn_down[0], final_norm_g, tm=tm, ts=ts)
```
